```python
import math
import jax, jax.numpy as jnp
from jax import lax
import numpy as np

D_MODEL = 1024
BATCH = 2
SEQ = 8192
DEPTH = 1
DEC_BATCH = 8
DEC_SEQ = 32
PAST_LEN = 4096

CHUNK = 64
MIX_WIDTH = D_MODEL
MLSTM_WIDTH = MIX_WIDTH // 2
MLSTM_HEADS = 4
QK_DIM = MLSTM_WIDTH // MLSTM_HEADS
V_DIM = MLSTM_WIDTH // MLSTM_HEADS
ATT_WIDTH = MIX_WIDTH - MLSTM_WIDTH
ATT_HEADS = 8
ATT_HEAD_DIM = ATT_WIDTH // ATT_HEADS
KV_HEADS = 2
GQA_GROUP = ATT_HEADS // KV_HEADS
WINDOW = 128
WINDOW_CHUNKS = WINDOW // CHUNK
D_FF = 4 * D_MODEL
PLE_DIM = 256
EPS = 1e-6
PROJ_SIZES = (MLSTM_WIDTH, MLSTM_WIDTH, MLSTM_WIDTH, MLSTM_WIDTH, MLSTM_HEADS, MLSTM_HEADS,
              ATT_WIDTH, KV_HEADS * ATT_HEAD_DIM, KV_HEADS * ATT_HEAD_DIM)
PROJ_WIDTH = sum(PROJ_SIZES)

kernel_name = "hymba_mlstm_swa_sink_alibi_stream_step"


def _rmsnorm(x, g):
    x32 = x.astype(jnp.float32)
    y = x32 * lax.rsqrt(jnp.mean(x32 * x32, axis=-1, keepdims=True) + EPS)
    return (y * g.astype(jnp.float32)).astype(x.dtype)


def _split_cols(proj):
    outs, off = [], 0
    for s in PROJ_SIZES:
        outs.append(proj[..., off:off + s])
        off += s
    return outs


def _alibi_slopes():
    return 2.0 ** (-(jnp.arange(1, ATT_HEADS + 1, dtype=jnp.float32) * (8.0 / ATT_HEADS)))


def _mlstm_chunkwise(q, k, v, i_pre, f_pre, c0, n0, m0, chunk):
    B, S, H, dk = q.shape
    dv = v.shape[-1]
    nc = S // chunk
    f32 = jnp.float32

    def to_chunks(a):
        a = a.astype(f32).reshape((B, nc, chunk, H) + a.shape[3:])
        return jnp.moveaxis(jnp.moveaxis(a, 1, 0), 3, 2)

    qc, kc, vc = to_chunks(q), to_chunks(k), to_chunks(v)
    ic, lfc = to_chunks(i_pre), to_chunks(jax.nn.log_sigmoid(f_pre.astype(f32)))
    causal = jnp.tril(jnp.ones((chunk, chunk), dtype=bool))

    def step(carry, xs):
        c, n, m = carry
        qb, kb, vb, ib, lfb = xs
        b = jnp.cumsum(lfb, axis=-1)
        a = b + m[..., None]
        dmat = jnp.where(causal, b[..., :, None] - b[..., None, :] + ib[..., None, :], -jnp.inf)
        mt = jnp.maximum(a, jnp.max(dmat, axis=-1))
        w_inter = jnp.exp(a - mt)
        s = jnp.einsum('bhtd,bhsd->bhts', qb, kb) * jnp.exp(dmat - mt[..., None])
        num = jnp.einsum('bhts,bhsv->bhtv', s, vb) + w_inter[..., None] * jnp.einsum('bhtd,bhdv->bhtv', qb, c)
        den = jnp.sum(s, axis=-1) + w_inter * jnp.einsum('bhtd,bhd->bht', qb, n)
        h = num / jnp.maximum(jnp.abs(den), jnp.exp(-mt))[..., None]
        m_new = mt[..., -1]
        w_c = jnp.exp(b[..., -1] + m - m_new)
        w_s = jnp.exp(b[..., -1:] - b + ib - m_new[..., None])
        c_new = w_c[..., None, None] * c + jnp.einsum('bhs,bhsd,bhsv->bhdv', w_s, kb, vb)
        n_new = w_c[..., None] * n + jnp.einsum('bhs,bhsd->bhd', w_s, kb)
        return (c_new, n_new, m_new), h

    (c1, n1, m1), hs = lax.scan(step, (c0.astype(f32), n0.astype(f32), m0.astype(f32)),
                                (qc, kc, vc, ic, lfc))
    hs = jnp.moveaxis(jnp.moveaxis(hs, 2, 3), 0, 1).reshape(B, S, H, dv)
    return hs.astype(q.dtype), c1, n1, m1


def _swa_attention(qb, kb, vb, q_pos, k_pos, sinks):
    f32 = jnp.float32
    s = jnp.einsum('bnqkgd,bnskd->bnkgqs', qb.astype(f32), kb.astype(f32)) * (ATT_HEAD_DIM ** -0.5)
    qp, kp = q_pos[:, :, None], k_pos[:, None, :]
    cdiff = qp // CHUNK - kp // CHUNK
    mask = (cdiff >= 0) & (cdiff <= WINDOW_CHUNKS) & (kp >= 0)
    dist = jnp.abs(qp - kp).astype(f32)
    slopes = _alibi_slopes().reshape(KV_HEADS, GQA_GROUP)
    s = s - slopes[None, None, :, :, None, None] * dist[None, :, None, None]
    s = jnp.where(mask[None, :, None, None], s, -jnp.inf)
    sink = sinks.astype(f32).reshape(KV_HEADS, GQA_GROUP)[None, None, :, :, None]
    mx = jnp.maximum(jnp.max(s, axis=-1), sink)
    p = jnp.exp(s - mx[..., None])
    den = jnp.sum(p, axis=-1) + jnp.exp(sink - mx)
    o = jnp.einsum('bnkgqs,bnskd->bnqkgd', p / den[..., None], vb.astype(f32))
    B, N, Lq = o.shape[:3]
    return o.reshape(B, N * Lq, ATT_WIDTH).astype(qb.dtype)


def _layer(x, p, c0, n0, m0, cache_k, cache_v, n_keep, prompt,
           g_mix, w_in, b_igate, b_fgate, g_head, attn_sinks, w_out,
           g_mlp, w_up, w_down, g_ple, w_ple_gate, w_ple_proj):
    B, S, _ = x.shape
    h = _rmsnorm(x, g_mix)
    q_m, k_m, v_m, o_m, i_pre, f_pre, q_a, k_a, v_a = _split_cols(h @ w_in)

    qm = q_m.reshape(B, S, MLSTM_HEADS, QK_DIM)
    km = k_m.reshape(B, S, MLSTM_HEADS, QK_DIM) * (QK_DIM ** -0.5)
    vm = v_m.reshape(B, S, MLSTM_HEADS, V_DIM)
    chunk = CHUNK if prompt else S
    h_til, c1, n1, m1 = _mlstm_chunkwise(qm, km, vm, i_pre + b_igate, f_pre + b_fgate,
                                         c0, n0, m0, chunk)
    h_m = jax.nn.sigmoid(o_m) * _rmsnorm(h_til, g_head.reshape(MLSTM_HEADS, V_DIM)).reshape(B, S, MLSTM_WIDTH)

    qa = q_a.reshape(B, S, KV_HEADS, GQA_GROUP, ATT_HEAD_DIM)
    ka = k_a.reshape(B, S, KV_HEADS, ATT_HEAD_DIM)
    va = v_a.reshape(B, S, KV_HEADS, ATT_HEAD_DIM)
    if prompt:
        nc = S // CHUNK
        pad = WINDOW_CHUNKS * CHUNK
        band = lambda a: jnp.concatenate(
            [a[:, j:j + nc] for j in range(WINDOW_CHUNKS + 1)], axis=2)
        kp = jnp.pad(ka, ((0, 0), (pad, 0), (0, 0), (0, 0))).reshape(B, nc + WINDOW_CHUNKS, CHUNK, KV_HEADS, ATT_HEAD_DIM)
        vp = jnp.pad(va, ((0, 0), (pad, 0), (0, 0), (0, 0))).reshape(B, nc + WINDOW_CHUNKS, CHUNK, KV_HEADS, ATT_HEAD_DIM)
        kb, vb = band(kp), band(vp)
        qb = qa.reshape(B, nc, CHUNK, KV_HEADS, GQA_GROUP, ATT_HEAD_DIM)
        q_pos = jnp.arange(S, dtype=jnp.int32).reshape(nc, CHUNK)
        k_pos = (jnp.arange(nc, dtype=jnp.int32) * CHUNK)[:, None] - pad + \
            jnp.arange((WINDOW_CHUNKS + 1) * CHUNK, dtype=jnp.int32)[None, :]
        k_all, v_all = ka, va
    else:
        k_all = jnp.concatenate([cache_k.astype(ka.dtype), ka], axis=1)
        v_all = jnp.concatenate([cache_v.astype(va.dtype), va], axis=1)
        kb, vb, qb = k_all[:, None], v_all[:, None], qa[:, None]
        q_pos = (PAST_LEN + jnp.arange(S, dtype=jnp.int32))[None]
        k_pos = (PAST_LEN - n_keep + jnp.arange(n_keep + S, dtype=jnp.int32))[None]
    o_a = _swa_attention(qb, kb, vb, q_pos, k_pos, attn_sinks)
    new_k, new_v = k_all[:, -n_keep:], v_all[:, -n_keep:]

    x = x + jnp.concatenate([h_m, o_a], axis=-1) @ w_out
    x = x + jnp.square(jax.nn.relu(_rmsnorm(x, g_mlp) @ w_up)) @ w_down
    x = x + jax.nn.sigmoid(_rmsnorm(x, g_ple) @ w_ple_gate) * (p @ w_ple_proj)
    return x, new_k, new_v, c1, n1, m1


def setup_inputs(seed: int = 0) -> dict:
    key = jax.random.key(seed)
    ks = jax.random.split(key, 24)
    nrm = lambda k, shape, scale: jax.random.normal(k, shape, jnp.float32) * scale
    w_keep = min(WINDOW, PAST_LEN)
    return {
        "x_prompt": nrm(ks[0], (BATCH, SEQ, D_MODEL), 1.0),
        "x_sample": nrm(ks[1], (DEC_BATCH, DEC_SEQ, D_MODEL), 1.0),
        "cache_swa_k": nrm(ks[2], (DEPTH, DEC_BATCH, w_keep, KV_HEADS, ATT_HEAD_DIM), 1.0),
        "cache_swa_v": nrm(ks[3], (DEPTH, DEC_BATCH, w_keep, KV_HEADS, ATT_HEAD_DIM), 1.0),
        "state_mlstm_c": nrm(ks[4], (DEPTH, DEC_BATCH, MLSTM_HEADS, QK_DIM, V_DIM), 0.1),
        "state_mlstm_n": nrm(ks[5], (DEPTH, DEC_BATCH, MLSTM_HEADS, QK_DIM), 0.1),
        "state_mlstm_m": nrm(ks[6], (DEPTH, DEC_BATCH, MLSTM_HEADS), 1.0),
        "p_prompt": nrm(ks[7], (DEPTH, BATCH, SEQ, PLE_DIM), 1.0),
        "p_sample": nrm(ks[8], (DEPTH, DEC_BATCH, DEC_SEQ, PLE_DIM), 1.0),
        "g_mix": 1.0 + nrm(ks[9], (DEPTH, D_MODEL), 0.02),
        "w_in": nrm(ks[10], (DEPTH, D_MODEL, PROJ_WIDTH), D_MODEL ** -0.5),
        "b_igate": nrm(ks[11], (DEPTH, MLSTM_HEADS), 0.1),
        "b_fgate": 3.0 + nrm(ks[12], (DEPTH, MLSTM_HEADS), 0.1),
        "g_head": 1.0 + nrm(ks[13], (DEPTH, MLSTM_WIDTH), 0.02),
        "attn_sinks": nrm(ks[14], (DEPTH, ATT_HEADS), 0.5),
        "w_out": nrm(ks[15], (DEPTH, MIX_WIDTH, D_MODEL), MIX_WIDTH ** -0.5),
        "g_mlp": 1.0 + nrm(ks[16], (DEPTH, D_MODEL), 0.02),
        "w_up": nrm(ks[17], (DEPTH, D_MODEL, D_FF), D_MODEL ** -0.5),
        "w_down": nrm(ks[18], (DEPTH, D_FF, D_MODEL), D_FF ** -0.5),
        "g_ple": 1.0 + nrm(ks[19], (DEPTH, D_MODEL), 0.02),
        "w_ple_gate": nrm(ks[20], (DEPTH, D_MODEL, D_MODEL), D_MODEL ** -0.5),
        "w_ple_proj": nrm(ks[21], (DEPTH, PLE_DIM, D_MODEL), PLE_DIM ** -0.5),
        "g_final": 1.0 + nrm(ks[22], (D_MODEL,), 0.02),
    }


def reference(x_prompt, x_sample, cache_swa_k, cache_swa_v, state_mlstm_c, state_mlstm_n,
              state_mlstm_m, p_prompt, p_sample, g_mix, w_in, b_igate, b_fgate, g_head,
              attn_sinks, w_out, g_mlp, w_up, w_down, g_ple, w_ple_gate, w_ple_proj, g_final):
    n_keep = cache_swa_k.shape[2]
    xp, xs = x_prompt, x_sample
    pk, pv, pc, pn, pm = [], [], [], [], []
    sk, sv, sc, sn, sm = [], [], [], [], []
    for l in range(DEPTH):
        lw = (g_mix[l], w_in[l], b_igate[l], b_fgate[l], g_head[l], attn_sinks[l], w_out[l],
              g_mlp[l], w_up[l], w_down[l], g_ple[l], w_ple_gate[l], w_ple_proj[l])
        zc = jnp.zeros((BATCH, MLSTM_HEADS, QK_DIM, V_DIM), jnp.float32)
        zn = jnp.zeros((BATCH, MLSTM_HEADS, QK_DIM), jnp.float32)
        zm = jnp.zeros((BATCH, MLSTM_HEADS), jnp.float32)
        xp, k1, v1, c1, n1, m1 = _layer(xp, p_prompt[l], zc, zn, zm, None, None, n_keep, True, *lw)
        xs, k2, v2, c2, n2, m2 = _layer(xs, p_sample[l], state_mlstm_c[l], state_mlstm_n[l],
                                        state_mlstm_m[l], cache_swa_k[l], cache_swa_v[l],
                                        n_keep, False, *lw)
        pk.append(k1); pv.append(v1); pc.append(c1); pn.append(n1); pm.append(m1)
        sk.append(k2); sv.append(v2); sc.append(c2); sn.append(n2); sm.append(m2)
    y_prompt = _rmsnorm(xp, g_final)
    y_sample = _rmsnorm(xs, g_final)
    return (y_prompt, y_sample,
            jnp.stack(pk), jnp.stack(pv), jnp.stack(pc), jnp.stack(pn), jnp.stack(pm),
            jnp.stack(sk), jnp.stack(sv), jnp.stack(sc), jnp.stack(sn), jnp.stack(sm))
```

```python
import functools

import jax
import jax.numpy as jnp
from jax import lax
from jax.experimental import pallas as pl
from jax.experimental.pallas import tpu as pltpu

D_MODEL = 1024
CHUNK = 64
MLSTM_HEADS = 4
QK_DIM = 128
V_DIM = 128
MLSTM_WIDTH = MLSTM_HEADS * QK_DIM
ATT_HEADS = 8
ATT_HEAD_DIM = 64
KV_HEADS = 2
GQA_GROUP = ATT_HEADS // KV_HEADS
ATT_WIDTH = ATT_HEADS * ATT_HEAD_DIM
KV_WIDTH = KV_HEADS * ATT_HEAD_DIM
WINDOW = 128
WINDOW_CHUNKS = WINDOW // CHUNK
D_FF = 4 * D_MODEL
PLE_DIM = 256
EPS = 1e-6
LANES = 128

COL_QM = 0
COL_KM = COL_QM + MLSTM_WIDTH
COL_VM = COL_KM + MLSTM_WIDTH
COL_OM = COL_VM + MLSTM_WIDTH
COL_QA = COL_OM + MLSTM_WIDTH
COL_KVA = COL_QA + ATT_WIDTH
COL_GATE = COL_KVA + 2 * KV_WIDTH
PROJ_PAD = COL_GATE + LANES

VMEM_LIMIT_BYTES = 56 * 1024 * 1024


def _bf16(a):
    return a.astype(jnp.bfloat16)


def _dot(a, b):
    return jnp.dot(a, b, preferred_element_type=jnp.float32)


def _dot_nt(a, b):
    return lax.dot_general(a, b, (((1,), (1,)), ((), ())), preferred_element_type=jnp.float32)


def _dot_tn(a, b):
    return lax.dot_general(a, b, (((0,), (0,)), ((), ())), preferred_element_type=jnp.float32)


def _rms(x, g):
    return x * lax.rsqrt(jnp.mean(x * x, axis=-1, keepdims=True) + EPS) * g


def _log_sigmoid(x):
    return jnp.minimum(x, 0.0) - jnp.log1p(jnp.exp(-jnp.abs(x)))


def _chunk_cumsum(a, chunk):
    row = lax.broadcasted_iota(jnp.int32, a.shape, 0) % chunk
    shift = 1
    while shift < chunk:
        a = a + jnp.where(row >= shift, pltpu.roll(a, shift, axis=0), 0.0)
        shift *= 2
    return a


def _mixer_kernel(x_ref, c0_ref, n0_ref, m0_ref, ck_ref, cv_ref,
                  gmix_ref, win_ref, gbias_ref, ghead_ref, sink_ref, wout_ref,
                  x1_ref, newk_ref, newv_ref, c1_ref, n1_ref, m1_ref,
                  qkv_s, om_s, qa_s, kbuf, vbuf, gate_s, hc_s, c_s, n_s, m_s,
                  *, tile, chunk, qblk, pos_base):
    i = pl.program_id(1)
    last = pl.num_programs(1) - 1

    @pl.when(i == 0)
    def _load_state():
        c_s[...] = c0_ref[0]
        n_s[...] = n0_ref[0]
        m_s[...] = m0_ref[0]
        kbuf[0:WINDOW, :] = ck_ref[0]
        vbuf[0:WINDOW, :] = cv_ref[0]

    x = x_ref[0]
    h = _bf16(_rms(x, gmix_ref[...]))
    qkv_s[:, 0:MLSTM_WIDTH] = _bf16(_dot(h, win_ref[:, COL_QM:COL_KM]))
    qkv_s[:, MLSTM_WIDTH:2 * MLSTM_WIDTH] = _bf16(_dot(h, win_ref[:, COL_KM:COL_VM]) * (QK_DIM ** -0.5))
    qkv_s[:, 2 * MLSTM_WIDTH:3 * MLSTM_WIDTH] = _bf16(_dot(h, win_ref[:, COL_VM:COL_OM]))
    om_s[...] = _dot(h, win_ref[:, COL_OM:COL_QA])
    qa_s[...] = _bf16(_dot(h, win_ref[:, COL_QA:COL_KVA]))
    kv = _dot(h, win_ref[:, COL_KVA:COL_GATE])
    kbuf[WINDOW:WINDOW + tile, :] = kv[:, 0:KV_WIDTH]
    vbuf[WINDOW:WINDOW + tile, :] = kv[:, KV_WIDTH:2 * KV_WIDTH]

    gates = _dot(h, win_ref[:, COL_GATE:PROJ_PAD]) + gbias_ref[...]
    cum_lf = _chunk_cumsum(_log_sigmoid(gates), chunk)
    lane = lax.broadcasted_iota(jnp.int32, gates.shape, 1)
    packed = jnp.where(lane < MLSTM_HEADS, gates, cum_lf)
    if tile < LANES:
        gate_s[...] = jnp.zeros(gate_s.shape, jnp.float32)
    gate_s[0:tile, :] = packed

    group = min(tile, LANES)
    n_groups = tile // group
    chunks_per_group = group // chunk
    qblks_per_group = group // qblk
    kwin = WINDOW + qblk

    causal = (lax.broadcasted_iota(jnp.int32, (chunk, chunk), 0)
              >= lax.broadcasted_iota(jnp.int32, (chunk, chunk), 1))
    q_iota = lax.broadcasted_iota(jnp.int32, (qblk, kwin), 0)
    k_iota = lax.broadcasted_iota(jnp.int32, (qblk, kwin), 1)

    def group_body(g, carry):
        g0 = pl.multiple_of(g * group, group)
        gblock = gate_s[pl.ds(g0, LANES), :]
        gblock_t = gblock.T

        for cc in range(chunks_per_group):
            r0 = pl.multiple_of(g0 + cc * chunk, chunk)
            lo = cc * chunk
            gcol = gblock[lo:lo + chunk, :]
            for hd in range(MLSTM_HEADS):
                i_c = gcol[:, hd:hd + 1]
                b_c = gcol[:, MLSTM_HEADS + hd:MLSTM_HEADS + hd + 1]
                i_r = gblock_t[hd:hd + 1, lo:lo + chunk]
                b_r = gblock_t[MLSTM_HEADS + hd:MLSTM_HEADS + hd + 1, lo:lo + chunk]
                m_prev = m_s[0:1, hd:hd + 1]
                c_prev = c_s[hd]
                n_prev = n_s[hd:hd + 1, :]

                q = qkv_s[pl.ds(r0, chunk), hd * QK_DIM:(hd + 1) * QK_DIM]
                k = qkv_s[pl.ds(r0, chunk), MLSTM_WIDTH + hd * QK_DIM:MLSTM_WIDTH + (hd + 1) * QK_DIM]
                v = qkv_s[pl.ds(r0, chunk), 2 * MLSTM_WIDTH + hd * V_DIM:2 * MLSTM_WIDTH + (hd + 1) * V_DIM]

                dmat = jnp.where(causal, b_c - b_r + i_r, -jnp.inf)
                mt = jnp.maximum(b_c + m_prev, jnp.max(dmat, axis=1, keepdims=True))
                w_inter = jnp.exp(b_c + m_prev - mt)
                s = _dot_nt(q, k) * jnp.exp(dmat - mt)
                num = _dot(_bf16(s), v) + w_inter * _dot(q, _bf16(c_prev))
                qn = jnp.sum(q.astype(jnp.float32) * n_prev, axis=1, keepdims=True)
                den = jnp.sum(s, axis=1, keepdims=True) + w_inter * qn
                hh = num / jnp.maximum(jnp.abs(den), jnp.exp(-mt))

                y = _rms(hh, ghead_ref[:, hd * V_DIM:(hd + 1) * V_DIM])
                om = om_s[pl.ds(r0, chunk), hd * V_DIM:(hd + 1) * V_DIM]
                hc_s[pl.ds(r0, chunk), hd * V_DIM:(hd + 1) * V_DIM] = _bf16(jax.nn.sigmoid(om) * y)

                m_new = mt[chunk - 1:chunk, :]
                b_last = b_c[chunk - 1:chunk, :]
                w_c = jnp.exp(b_last + m_prev - m_new)
                w_s = jnp.exp(b_last - b_c + i_c - m_new)
                kw = k.astype(jnp.float32) * w_s
                c_s[hd] = w_c * c_prev + _dot_tn(_bf16(kw), v)
                n_s[hd:hd + 1, :] = w_c * n_prev + jnp.sum(kw, axis=0, keepdims=True)
                m_s[0:1, hd:hd + 1] = m_new

        for qq in range(qblks_per_group):
            r0 = pl.multiple_of(g0 + qq * qblk, qblk)
            base = pos_base + i * tile + r0
            q_pos = base + q_iota
            k_pos = base - WINDOW + k_iota
            cdiff = jnp.right_shift(q_pos, 6) - jnp.right_shift(k_pos, 6)
            mask = (cdiff >= 0) & (cdiff <= WINDOW_CHUNKS) & (k_pos >= 0)
            dist = jnp.abs(q_pos - k_pos).astype(jnp.float32)
            for kvh in range(KV_HEADS):
                kw_ = _bf16(kbuf[pl.ds(r0, kwin), kvh * ATT_HEAD_DIM:(kvh + 1) * ATT_HEAD_DIM])
                vw_ = _bf16(vbuf[pl.ds(r0, kwin), kvh * ATT_HEAD_DIM:(kvh + 1) * ATT_HEAD_DIM])
                for gi in range(GQA_GROUP):
                    hd = kvh * GQA_GROUP + gi
                    slope = 2.0 ** (-(hd + 1) * (8.0 / ATT_HEADS))
                    qh = qa_s[pl.ds(r0, qblk), hd * ATT_HEAD_DIM:(hd + 1) * ATT_HEAD_DIM]
                    sc = _dot_nt(qh, kw_) * (ATT_HEAD_DIM ** -0.5) - slope * dist
                    sc = jnp.where(mask, sc, -jnp.inf)
                    sink = sink_ref[0:1, hd:hd + 1]
                    mx = jnp.maximum(jnp.max(sc, axis=1, keepdims=True), sink)
                    p = jnp.exp(sc - mx)
                    den = jnp.sum(p, axis=1, keepdims=True) + jnp.exp(sink - mx)
                    o = _dot(_bf16(p), vw_) / den
                    hc_s[pl.ds(r0, qblk),
                         MLSTM_WIDTH + hd * ATT_HEAD_DIM:MLSTM_WIDTH + (hd + 1) * ATT_HEAD_DIM] = _bf16(o)
        return carry

    lax.fori_loop(0, n_groups, group_body, 0)

    x1_ref[0] = x + _dot(hc_s[...], wout_ref[...])

    k_tail = kbuf[tile:tile + WINDOW, :]
    v_tail = vbuf[tile:tile + WINDOW, :]
    kbuf[0:WINDOW, :] = k_tail
    vbuf[0:WINDOW, :] = v_tail

    @pl.when(i == last)
    def _store_state():
        newk_ref[0] = k_tail
        newv_ref[0] = v_tail
        c1_ref[0] = c_s[...]
        n1_ref[0] = n_s[...]
        m1_ref[0] = m_s[...]


def _mixer(x, c0, n0, m0, ck, cv, gmix, win, gbias, ghead, sinks, wout,
           *, tile, chunk, qblk, pos_base):
    nb, seq, _ = x.shape
    n_tiles = seq // tile
    assert seq % tile == 0 and tile % chunk == 0 and tile % qblk == 0
    assert tile % LANES == 0 or tile < LANES
    assert (chunk & (chunk - 1)) == 0
    gate_rows = max(tile, LANES)

    per_seq = lambda b, i: (b, 0, 0)
    per_seq4 = lambda b, i: (b, 0, 0, 0)
    const2 = lambda b, i: (0, 0)
    f32 = jnp.float32
    kern = functools.partial(_mixer_kernel, tile=tile, chunk=chunk, qblk=qblk, pos_base=pos_base)
    return pl.pallas_call(
        kern,
        grid=(nb, n_tiles),
        in_specs=[
            pl.BlockSpec((1, tile, D_MODEL), lambda b, i: (b, i, 0)),
            pl.BlockSpec((1, MLSTM_HEADS, QK_DIM, V_DIM), per_seq4),
            pl.BlockSpec((1, MLSTM_HEADS, QK_DIM), per_seq),
            pl.BlockSpec((1, 1, LANES), per_seq),
            pl.BlockSpec((1, WINDOW, KV_WIDTH), per_seq),
            pl.BlockSpec((1, WINDOW, KV_WIDTH), per_seq),
            pl.BlockSpec((1, D_MODEL), const2),
            pl.BlockSpec((D_MODEL, PROJ_PAD), const2),
            pl.BlockSpec((1, LANES), const2),
            pl.BlockSpec((1, MLSTM_WIDTH), const2),
            pl.BlockSpec((1, LANES), const2),
            pl.BlockSpec((D_MODEL, D_MODEL), const2),
        ],
        out_specs=[
            pl.BlockSpec((1, tile, D_MODEL), lambda b, i: (b, i, 0)),
            pl.BlockSpec((1, WINDOW, KV_WIDTH), per_seq),
            pl.BlockSpec((1, WINDOW, KV_WIDTH), per_seq),
            pl.BlockSpec((1, MLSTM_HEADS, QK_DIM, V_DIM), per_seq4),
            pl.BlockSpec((1, MLSTM_HEADS, QK_DIM), per_seq),
            pl.BlockSpec((1, 1, LANES), per_seq),
        ],
        out_shape=[
            jax.ShapeDtypeStruct((nb, seq, D_MODEL), f32),
            jax.ShapeDtypeStruct((nb, WINDOW, KV_WIDTH), f32),
            jax.ShapeDtypeStruct((nb, WINDOW, KV_WIDTH), f32),
            jax.ShapeDtypeStruct((nb, MLSTM_HEADS, QK_DIM, V_DIM), f32),
            jax.ShapeDtypeStruct((nb, MLSTM_HEADS, QK_DIM), f32),
            jax.ShapeDtypeStruct((nb, 1, LANES), f32),
        ],
        scratch_shapes=[
            pltpu.VMEM((tile, 3 * MLSTM_WIDTH), jnp.bfloat16),
            pltpu.VMEM((tile, MLSTM_WIDTH), f32),
            pltpu.VMEM((tile, ATT_WIDTH), jnp.bfloat16),
            pltpu.VMEM((WINDOW + tile, KV_WIDTH), f32),
            pltpu.VMEM((WINDOW + tile, KV_WIDTH), f32),
            pltpu.VMEM((gate_rows, LANES), f32),
            pltpu.VMEM((tile, D_MODEL), jnp.bfloat16),
            pltpu.VMEM((MLSTM_HEADS, QK_DIM, V_DIM), f32),
            pltpu.VMEM((MLSTM_HEADS, QK_DIM), f32),
            pltpu.VMEM((1, LANES), f32),
        ],
        compiler_params=pltpu.CompilerParams(
            dimension_semantics=("arbitrary", "arbitrary"),
            vmem_limit_bytes=VMEM_LIMIT_BYTES),
        name=f"mixer_t{tile}",
    )(x, c0, n0, m0, ck, cv, gmix, win, gbias, ghead, sinks, wout)


def _ffn_kernel(x_ref, p_ref, gmlp_ref, wup_ref, wdown_ref, gple_ref, wgate_ref, wproj_ref,
                gfin_ref, y_ref, *, ff_slab):
    x = x_ref[...]
    h = _bf16(_rms(x, gmlp_ref[...]))
    acc = x
    for j in range(D_FF // ff_slab):
        u = _dot(h, wup_ref[:, j * ff_slab:(j + 1) * ff_slab])
        a = jnp.square(jnp.maximum(u, 0.0))
        acc = acc + _dot(_bf16(a), wdown_ref[j * ff_slab:(j + 1) * ff_slab, :])
    hp = _bf16(_rms(acc, gple_ref[...]))
    gate = jax.nn.sigmoid(_dot(hp, wgate_ref[...]))
    emb = _dot(_bf16(p_ref[...]), wproj_ref[...])
    y_ref[...] = _rms(acc + gate * emb, gfin_ref[...])


def _ffn(x, p, gmlp, wup, wdown, gple, wgate, wproj, gfin, *, tile, ff_slab=1024):
    rows = x.shape[0]
    assert rows % tile == 0 and D_FF % ff_slab == 0
    const2 = lambda i: (0, 0)
    resident = functools.partial(pl.BlockSpec, index_map=const2, pipeline_mode=pl.Buffered(1))
    kern = functools.partial(_ffn_kernel, ff_slab=ff_slab)
    return pl.pallas_call(
        kern,
        grid=(rows // tile,),
        in_specs=[
            pl.BlockSpec((tile, D_MODEL), lambda i: (i, 0)),
            pl.BlockSpec((tile, PLE_DIM), lambda i: (i, 0)),
            pl.BlockSpec((1, D_MODEL), const2),
            resident((D_MODEL, D_FF)),
            resident((D_FF, D_MODEL)),
            pl.BlockSpec((1, D_MODEL), const2),
            resident((D_MODEL, D_MODEL)),
            resident((PLE_DIM, D_MODEL)),
            pl.BlockSpec((1, D_MODEL), const2),
        ],
        out_specs=pl.BlockSpec((tile, D_MODEL), lambda i: (i, 0)),
        out_shape=jax.ShapeDtypeStruct((rows, D_MODEL), jnp.float32),
        compiler_params=pltpu.CompilerParams(
            dimension_semantics=("arbitrary",),
            vmem_limit_bytes=VMEM_LIMIT_BYTES),
        name=f"ffn_t{tile}",
    )(x, p, gmlp, wup, wdown, gple, wgate, wproj, gfin)


def _pad_lanes(a):
    return jnp.pad(a, [(0, 0)] * (a.ndim - 1) + [(0, LANES - a.shape[-1])])


def kernel(x_prompt, x_sample, cache_swa_k, cache_swa_v, state_mlstm_c, state_mlstm_n, state_mlstm_m, p_prompt, p_sample, g_mix, w_in, b_igate, b_fgate, g_head, attn_sinks, w_out, g_mlp, w_up, w_down, g_ple, w_ple_gate, w_ple_proj, g_final):
    depth = w_in.shape[0]
    n_keep = cache_swa_k.shape[2]
    assert n_keep == WINDOW
    nb_p, seq_p, _ = x_prompt.shape
    nb_s, seq_s, _ = x_sample.shape
    past_len = 4096
    f32 = jnp.float32

    xp, xs = x_prompt, x_sample
    outs_p, outs_s = [], []
    for l in range(depth):
        w = w_in[l]
        off_i = 4 * MLSTM_WIDTH
        off_qa = off_i + 2 * MLSTM_HEADS
        win = jnp.concatenate(
            [w[:, :off_i], w[:, off_qa:], w[:, off_i:off_qa],
             jnp.zeros((D_MODEL, LANES - 2 * MLSTM_HEADS), w.dtype)], axis=1).astype(jnp.bfloat16)
        gbias = _pad_lanes(jnp.concatenate([b_igate[l], b_fgate[l]])[None, :])
        sinks = _pad_lanes(attn_sinks[l][None, :])
        mix_w = (g_mix[l][None, :], win, gbias, g_head[l][None, :], sinks, w_out[l].astype(jnp.bfloat16))
        ffn_w = (g_mlp[l][None, :], w_up[l].astype(jnp.bfloat16), w_down[l].astype(jnp.bfloat16),
                 g_ple[l][None, :], w_ple_gate[l].astype(jnp.bfloat16),
                 w_ple_proj[l].astype(jnp.bfloat16))
        assert depth == 1
        gfin = g_final[None, :]

        zc = jnp.zeros((nb_p, MLSTM_HEADS, QK_DIM, V_DIM), f32)
        zn = jnp.zeros((nb_p, MLSTM_HEADS, QK_DIM), f32)
        zm = jnp.zeros((nb_p, 1, LANES), f32)
        zkv = jnp.zeros((nb_p, WINDOW, KV_WIDTH), f32)
        x1, k1, v1, c1, n1, m1 = _mixer(xp, zc, zn, zm, zkv, zkv, *mix_w,
                                        tile=512, chunk=CHUNK, qblk=128, pos_base=0)
        xp = _ffn(x1.reshape(nb_p * seq_p, D_MODEL), p_prompt[l].reshape(nb_p * seq_p, PLE_DIM),
                  *ffn_w, gfin, tile=512).reshape(nb_p, seq_p, D_MODEL)
        outs_p.append((k1.reshape(nb_p, WINDOW, KV_HEADS, ATT_HEAD_DIM),
                       v1.reshape(nb_p, WINDOW, KV_HEADS, ATT_HEAD_DIM),
                       c1, n1, m1[:, 0, :MLSTM_HEADS]))

        x1, k2, v2, c2, n2, m2 = _mixer(
            xs, state_mlstm_c[l], state_mlstm_n[l], _pad_lanes(state_mlstm_m[l])[:, None, :],
            cache_swa_k[l].reshape(nb_s, WINDOW, KV_WIDTH), cache_swa_v[l].reshape(nb_s, WINDOW, KV_WIDTH),
            *mix_w, tile=seq_s, chunk=seq_s, qblk=seq_s, pos_base=past_len)
        xs = _ffn(x1.reshape(nb_s * seq_s, D_MODEL), p_sample[l].reshape(nb_s * seq_s, PLE_DIM),
                  *ffn_w, gfin, tile=nb_s * seq_s).reshape(nb_s, seq_s, D_MODEL)
        outs_s.append((k2.reshape(nb_s, WINDOW, KV_HEADS, ATT_HEAD_DIM),
                       v2.reshape(nb_s, WINDOW, KV_HEADS, ATT_HEAD_DIM),
                       c2, n2, m2[:, 0, :MLSTM_HEADS]))

    stack = lambda outs, j: jnp.stack([o[j] for o in outs])
    return (xp, xs,
            stack(outs_p, 0), stack(outs_p, 1), stack(outs_p, 2), stack(outs_p, 3), stack(outs_p, 4),
            stack(outs_s, 0), stack(outs_s, 1), stack(outs_s, 2), stack(outs_s, 3), stack(outs_s, 4))
```
